```python
import jax, jax.numpy as jnp
from jax import lax
import numpy as np

D_MODEL = 1024
BATCH = 8
SEQ = 2048
DEPTH = 2
DEC_BATCH = 128
DEC_SEQ = 8
PAST_LEN = 16384
PAGE_SIZE = 128

N_MIXERS = 2
N_CONV_LAYERS = (DEPTH + 1) // 2
N_POOL_LAYERS = DEPTH // 2
N_META = 16
CONV_W = 31
POOL_WINDOWS = (2, 4, 8, 16)
N_GROUPS = len(POOL_WINDOWS)
GROUP_W = D_MODEL // N_GROUPS
MAX_W = max(POOL_WINDOWS)
D_FF = 4 * D_MODEL
EPS = 1e-6

kernel_name = "hybrid_conformer_conv_pool_decoder_step"


def rmsnorm(x, g):
    xf = x.astype(jnp.float32)
    y = xf * lax.rsqrt(jnp.mean(xf * xf, axis=-1, keepdims=True) + EPS)
    return (y * g.astype(jnp.float32)).astype(x.dtype)


def layernorm(x, g, b):
    xf = x.astype(jnp.float32)
    mu = jnp.mean(xf, axis=-1, keepdims=True)
    var = jnp.mean(jnp.square(xf - mu), axis=-1, keepdims=True)
    y = (xf - mu) * lax.rsqrt(var + EPS)
    return (y * g.astype(jnp.float32) + b.astype(jnp.float32)).astype(x.dtype)


def conv_module(h, prev, w_pw1, b_pw1, w_dw, b_dw, ln_g, ln_b, w_pw2, b_pw2):
    u = jnp.einsum('btd,de->bte', h, w_pw1) + b_pw1
    a, gate = jnp.split(u, 2, axis=-1)
    v = a * jax.nn.sigmoid(gate)
    ext = jnp.concatenate([prev.astype(v.dtype), v], axis=1)
    c = lax.conv_general_dilated(
        ext, w_dw[:, None, :].astype(ext.dtype), window_strides=(1,), padding='VALID',
        dimension_numbers=('NWC', 'WIO', 'NWC'), feature_group_count=D_MODEL) + b_dw
    z = jax.nn.silu(layernorm(c, ln_g, ln_b))
    out = jnp.einsum('btd,de->bte', z, w_pw2) + b_pw2
    return out, ext[:, -(CONV_W - 1):]


def pool_mixer(h, prev, pos0, w_grp, b_grp, scale):
    B, T, _ = h.shape
    ext = jnp.concatenate([prev.astype(h.dtype), h], axis=1)
    cs = jnp.cumsum(ext.astype(jnp.float32), axis=1)
    cs = jnp.pad(cs, ((0, 0), (1, 0), (0, 0)))
    pos = pos0 + jnp.arange(T)
    pooled = []
    for g, w in enumerate(POOL_WINDOWS):
        sl = slice(g * GROUP_W, (g + 1) * GROUP_W)
        s = cs[:, MAX_W:MAX_W + T, sl] - cs[:, MAX_W - w:MAX_W - w + T, sl]
        cnt = jnp.minimum(w, pos + 1).astype(jnp.float32)
        pooled.append(s / cnt[None, :, None])
    p = jnp.concatenate(pooled, axis=-1).astype(h.dtype) - h
    pg = p.reshape(B, T, N_GROUPS, GROUP_W)
    out = jnp.einsum('btgc,gcd->btgd', pg, w_grp) + b_grp
    out = out.reshape(B, T, D_MODEL) * scale
    return out, ext[:, -(MAX_W - 1):]


def trunk(x, conv_prev, pool_prev, pos0, norm_mix_g, norm_mlp_g,
          conv_w_pw1, conv_b_pw1, conv_w_dw, conv_b_dw, conv_ln_g, conv_ln_b,
          conv_w_pw2, conv_b_pw2, pool_w, pool_b, pool_scale, mlp_w1, mlp_w2, final_g):
    new_conv, new_pool = [], []
    for i in range(DEPTH):
        h = rmsnorm(x, norm_mix_g[i])
        j = i // N_MIXERS
        if i % N_MIXERS == 0:
            out, st = conv_module(h, conv_prev[j], conv_w_pw1[j], conv_b_pw1[j], conv_w_dw[j],
                                  conv_b_dw[j], conv_ln_g[j], conv_ln_b[j], conv_w_pw2[j], conv_b_pw2[j])
            new_conv.append(st)
        else:
            out, st = pool_mixer(h, pool_prev[j], pos0, pool_w[j], pool_b[j], pool_scale[j])
            new_pool.append(st)
        x = x + out
        h = rmsnorm(x, norm_mlp_g[i])
        x = x + jnp.einsum('btf,fd->btd', jnp.square(jax.nn.relu(jnp.einsum('btd,df->btf', h, mlp_w1[i]))), mlp_w2[i])
    return rmsnorm(x, final_g), jnp.stack(new_conv), jnp.stack(new_pool)


def setup_inputs(seed: int = 0) -> dict:
    key = jax.random.key(seed)
    ks = jax.random.split(key, 24)
    n = lambda k, shape, s: jax.random.normal(k, shape, jnp.float32) * s
    D = D_MODEL
    return {
        "x_prompt": n(ks[0], (BATCH, SEQ, D), 1.0),
        "x_sample": n(ks[1], (DEC_BATCH, DEC_SEQ, D), 1.0),
        "state_conv": n(ks[2], (N_CONV_LAYERS, DEC_BATCH, CONV_W - 1, D), 0.5),
        "state_pool": n(ks[3], (N_POOL_LAYERS, DEC_BATCH, MAX_W - 1, D), 1.0),
        "meta_tokens": n(ks[4], (N_META, D), 1.0),
        "norm_mix_g": 1.0 + n(ks[5], (DEPTH, D), 0.05),
        "norm_mlp_g": 1.0 + n(ks[6], (DEPTH, D), 0.05),
        "conv_w_pw1": n(ks[7], (N_CONV_LAYERS, D, 2 * D), D ** -0.5),
        "conv_b_pw1": n(ks[8], (N_CONV_LAYERS, 2 * D), 0.02),
        "conv_w_dw": n(ks[9], (N_CONV_LAYERS, CONV_W, D), CONV_W ** -0.5),
        "conv_b_dw": n(ks[10], (N_CONV_LAYERS, D), 0.02),
        "conv_ln_g": 1.0 + n(ks[11], (N_CONV_LAYERS, D), 0.05),
        "conv_ln_b": n(ks[12], (N_CONV_LAYERS, D), 0.02),
        "conv_w_pw2": n(ks[13], (N_CONV_LAYERS, D, D), D ** -0.5),
        "conv_b_pw2": n(ks[14], (N_CONV_LAYERS, D), 0.02),
        "pool_w": n(ks[15], (N_POOL_LAYERS, N_GROUPS, GROUP_W, GROUP_W), GROUP_W ** -0.5),
        "pool_b": n(ks[16], (N_POOL_LAYERS, N_GROUPS, GROUP_W), 0.02),
        "pool_scale": 0.5 + n(ks[17], (N_POOL_LAYERS, D), 0.05),
        "mlp_w1": n(ks[18], (DEPTH, D, D_FF), D ** -0.5),
        "mlp_w2": n(ks[19], (DEPTH, D_FF, D), 0.5 * D_FF ** -0.5),
        "final_g": 1.0 + n(ks[20], (D,), 0.05),
    }


def reference(x_prompt, x_sample, state_conv, state_pool, meta_tokens, norm_mix_g, norm_mlp_g,
              conv_w_pw1, conv_b_pw1, conv_w_dw, conv_b_dw, conv_ln_g, conv_ln_b,
              conv_w_pw2, conv_b_pw2, pool_w, pool_b, pool_scale, mlp_w1, mlp_w2, final_g):
    weights = (norm_mix_g, norm_mlp_g, conv_w_pw1, conv_b_pw1, conv_w_dw, conv_b_dw,
               conv_ln_g, conv_ln_b, conv_w_pw2, conv_b_pw2, pool_w, pool_b, pool_scale,
               mlp_w1, mlp_w2, final_g)
    meta = jnp.broadcast_to(meta_tokens.astype(x_prompt.dtype)[None], (x_prompt.shape[0], N_META, D_MODEL))
    xp = jnp.concatenate([meta, x_prompt], axis=1)
    conv0 = jnp.zeros((N_CONV_LAYERS, x_prompt.shape[0], CONV_W - 1, D_MODEL), x_prompt.dtype)
    pool0 = jnp.zeros((N_POOL_LAYERS, x_prompt.shape[0], MAX_W - 1, D_MODEL), x_prompt.dtype)
    yp, new_conv_prompt, new_pool_prompt = trunk(xp, conv0, pool0, 0, *weights)
    y_prompt = yp[:, N_META:]
    y_sample, new_conv_sample, new_pool_sample = trunk(x_sample, state_conv, state_pool, PAST_LEN, *weights)
    return (y_prompt, y_sample, new_conv_prompt, new_conv_sample, new_pool_prompt, new_pool_sample)
```

```python
import functools

import jax
import jax.numpy as jnp
from jax import lax
from jax.experimental import pallas as pl
from jax.experimental.pallas import tpu as pltpu

D_MODEL = 1024
CONV_W = 31
CONV_HIST = CONV_W - 1
POOL_WINDOWS = (2, 4, 8, 16)
GROUP_W = D_MODEL // len(POOL_WINDOWS)
POOL_HIST = max(POOL_WINDOWS) - 1
EPS = 1e-6

SUBLANES = 8
CONV_HALO = -(-CONV_HIST // SUBLANES) * SUBLANES
POOL_HALO = -(-POOL_HIST // SUBLANES) * SUBLANES
V7X_VMEM_BYTES = 64 * 1024 * 1024


def _rmsnorm(x, g):
    y = x * lax.rsqrt(jnp.mean(x * x, axis=-1, keepdims=True) + EPS)
    return y * g


def _layernorm(x, g, b):
    mu = jnp.mean(x, axis=-1, keepdims=True)
    xc = x - mu
    var = jnp.mean(xc * xc, axis=-1, keepdims=True)
    return xc * lax.rsqrt(var + EPS) * g + b


def _mlp(x, g, w1_ref, w2_ref):
    h = _rmsnorm(x, g).astype(jnp.bfloat16)
    a = jnp.dot(h, w1_ref[...], preferred_element_type=jnp.float32)
    a = jnp.square(jnp.maximum(a, 0.0)).astype(jnp.bfloat16)
    return x + jnp.dot(a, w2_ref[...], preferred_element_type=jnp.float32)


def _layer0_kernel(x_ref, prev_ref, g_mix_ref, w_pw1_ref, b_pw1_ref, w_dw_ref, b_dw_ref,
                   ln_g_ref, ln_b_ref, w_pw2_ref, b_pw2_ref, g_mlp_ref, w1_ref, w2_ref,
                   out_ref, newconv_ref, ext_ref, *, sb, tm):
    ti = pl.program_id(1)
    rows = sb * tm
    d = D_MODEL

    @pl.when(ti == 0)
    def _():
        ext_ref[:, 0:CONV_HALO, :] = jnp.zeros((sb, CONV_HALO, d), jnp.float32)
        ext_ref[:, CONV_HALO + tm:CONV_HALO + tm + SUBLANES, :] = jnp.zeros((sb, SUBLANES, d), jnp.float32)
        ext_ref[:, CONV_HALO - CONV_HIST:CONV_HALO, :] = prev_ref[...]

    x = x_ref[...].reshape(rows, d)
    h = _rmsnorm(x, g_mix_ref[...]).astype(jnp.bfloat16)
    u = jnp.dot(h, w_pw1_ref[...], preferred_element_type=jnp.float32) + b_pw1_ref[...]
    v = u[:, :d] * jax.nn.sigmoid(u[:, d:])
    ext_ref[:, CONV_HALO:CONV_HALO + tm, :] = v.reshape(sb, tm, d)

    off = CONV_HALO - CONV_HIST
    c = None
    for s in range(SUBLANES):
        p = None
        for q in range((CONV_W + off + SUBLANES - 1) // SUBLANES):
            k = SUBLANES * q + s - off
            if 0 <= k < CONV_W:
                term = ext_ref[:, SUBLANES * q:SUBLANES * q + tm + SUBLANES, :] * w_dw_ref[k:k + 1, :]
                p = term if p is None else p + term
        shifted = p[:, s:s + tm, :]
        c = shifted if c is None else c + shifted
    c = c.reshape(rows, d) + b_dw_ref[...]

    z = _layernorm(c, ln_g_ref[...], ln_b_ref[...])
    z = (z * jax.nn.sigmoid(z)).astype(jnp.bfloat16)
    x = x + jnp.dot(z, w_pw2_ref[...], preferred_element_type=jnp.float32) + b_pw2_ref[...]
    x = _mlp(x, g_mlp_ref[...], w1_ref, w2_ref)
    out_ref[...] = x.reshape(sb, tm, d)

    newconv_ref[...] = ext_ref[:, CONV_HALO + tm - CONV_HIST:CONV_HALO + tm, :]
    ext_ref[:, 0:CONV_HALO, :] = ext_ref[:, tm:tm + CONV_HALO, :]


def _layer1_kernel(x_ref, prev_ref, g_mix_ref, w_grp_ref, b_grp_ref, scale_ref, g_mlp_ref,
                   w1_ref, w2_ref, g_final_ref, y_ref, newpool_ref, ext_ref, *, sb, tm, pos0):
    ti = pl.program_id(1)
    rows = sb * tm
    d = D_MODEL

    @pl.when(ti == 0)
    def _():
        ext_ref[:, 0:POOL_HALO, :] = jnp.zeros((sb, POOL_HALO, d), jnp.float32)
        ext_ref[:, POOL_HALO - POOL_HIST:POOL_HALO, :] = prev_ref[...]

    x = x_ref[...].reshape(rows, d)
    h = _rmsnorm(x, g_mix_ref[...])
    ext_ref[:, POOL_HALO:POOL_HALO + tm, :] = h.reshape(sb, tm, d)

    pos = pos0 + ti * tm + lax.broadcasted_iota(jnp.int32, (tm, 1), 0)
    outs = []
    for g, w in enumerate(POOL_WINDOWS):
        lo, hi = g * GROUP_W, (g + 1) * GROUP_W
        s = ext_ref[:, POOL_HALO:POOL_HALO + tm, lo:hi]
        for i in range(1, w):
            s = s + ext_ref[:, POOL_HALO - i:POOL_HALO - i + tm, lo:hi]
        inv_cnt = 1.0 / jnp.minimum(w, pos + 1).astype(jnp.float32)
        p = (s * inv_cnt).reshape(rows, GROUP_W) - h[:, lo:hi]
        outs.append(jnp.dot(p.astype(jnp.bfloat16), w_grp_ref[g], preferred_element_type=jnp.float32))
    mix = (jnp.concatenate(outs, axis=-1) + b_grp_ref[...]) * scale_ref[...]
    x = x + mix
    x = _mlp(x, g_mlp_ref[...], w1_ref, w2_ref)
    y_ref[...] = _rmsnorm(x, g_final_ref[...]).reshape(sb, tm, d)

    newpool_ref[...] = ext_ref[:, POOL_HALO + tm - POOL_HIST:POOL_HALO + tm, :]
    ext_ref[:, 0:POOL_HALO, :] = ext_ref[:, tm:tm + POOL_HALO, :]


def _resident(shape):
    zeros = (0,) * len(shape)
    return pl.BlockSpec(shape, lambda b, t: zeros, pipeline_mode=pl.Buffered(1))


def _seq_specs(nb, seq, hist, sb, tm, shared_prev):
    d = D_MODEL
    x_spec = pl.BlockSpec((sb, tm, d), lambda b, t: (b, t, 0))
    prev_spec = pl.BlockSpec((sb, hist, d), (lambda b, t: (0, 0, 0)) if shared_prev else (lambda b, t: (b, 0, 0)))
    state_spec = pl.BlockSpec((sb, hist, d), lambda b, t: (b, 0, 0))
    out_shape = (jax.ShapeDtypeStruct((nb, seq, d), jnp.float32),
                 jax.ShapeDtypeStruct((nb, hist, d), jnp.float32))
    return x_spec, prev_spec, state_spec, out_shape


def _vmem_limit(resident_bytes, sb, tm, widest):
    act = sb * tm * 4
    est = resident_bytes + 6 * act * D_MODEL + 3 * act * widest
    return int(min(est + (16 << 20), V7X_VMEM_BYTES - (4 << 20)))


def _layer0(x, prev, params, *, sb, tm, shared_prev=False, name):
    nb, seq, d = x.shape
    assert nb % sb == 0 and seq % tm == 0 and tm % SUBLANES == 0
    assert shared_prev == (prev.shape[0] != nb)
    x_spec, prev_spec, state_spec, out_shape = _seq_specs(nb, seq, CONV_HIST, sb, tm, shared_prev)
    resident = sum(p.size * p.dtype.itemsize for p in params)
    return pl.pallas_call(
        functools.partial(_layer0_kernel, sb=sb, tm=tm),
        grid=(nb // sb, seq // tm),
        in_specs=[x_spec, prev_spec] + [_resident(p.shape) for p in params],
        out_specs=(x_spec, state_spec),
        out_shape=out_shape,
        scratch_shapes=[pltpu.VMEM((sb, CONV_HALO + tm + SUBLANES, d), jnp.float32)],
        compiler_params=pltpu.CompilerParams(
            dimension_semantics=("arbitrary", "arbitrary"),
            vmem_limit_bytes=_vmem_limit(resident, sb, tm, 4 * D_MODEL)),
        name=name,
    )(x, prev, *params)


def _layer1(x, prev, params, *, sb, tm, pos0, shared_prev=False, name):
    nb, seq, d = x.shape
    assert nb % sb == 0 and seq % tm == 0 and tm % SUBLANES == 0
    assert shared_prev == (prev.shape[0] != nb)
    x_spec, prev_spec, state_spec, out_shape = _seq_specs(nb, seq, POOL_HIST, sb, tm, shared_prev)
    resident = sum(p.size * p.dtype.itemsize for p in params)
    return pl.pallas_call(
        functools.partial(_layer1_kernel, sb=sb, tm=tm, pos0=pos0),
        grid=(nb // sb, seq // tm),
        in_specs=[x_spec, prev_spec] + [_resident(p.shape) for p in params],
        out_specs=(x_spec, state_spec),
        out_shape=out_shape,
        scratch_shapes=[pltpu.VMEM((sb, POOL_HALO + tm, d), jnp.float32)],
        compiler_params=pltpu.CompilerParams(
            dimension_semantics=("arbitrary", "arbitrary"),
            vmem_limit_bytes=_vmem_limit(resident, sb, tm, 4 * D_MODEL)),
        name=name,
    )(x, prev, *params)


PROMPT_TILE = 256
SAMPLE_SEQS_PER_TILE = 16


def kernel(x_prompt, x_sample, state_conv, state_pool, meta_tokens, norm_mix_g, norm_mlp_g, conv_w_pw1, conv_b_pw1, conv_w_dw, conv_b_dw, conv_ln_g, conv_ln_b, conv_w_pw2, conv_b_pw2, pool_w, pool_b, pool_scale, mlp_w1, mlp_w2, final_g):
    assert norm_mix_g.shape[0] == 2 and conv_w_pw1.shape[0] == 1 and pool_w.shape[0] == 1
    bf16 = jnp.bfloat16
    d = D_MODEL
    row = lambda a: a.reshape(1, -1)
    p0 = (row(norm_mix_g[0]), conv_w_pw1[0].astype(bf16), row(conv_b_pw1[0]), conv_w_dw[0],
          row(conv_b_dw[0]), row(conv_ln_g[0]), row(conv_ln_b[0]), conv_w_pw2[0].astype(bf16),
          row(conv_b_pw2[0]), row(norm_mlp_g[0]), mlp_w1[0].astype(bf16), mlp_w2[0].astype(bf16))
    p1 = (row(norm_mix_g[1]), pool_w[0].astype(bf16), row(pool_b[0]), row(pool_scale[0]),
          row(norm_mlp_g[1]), mlp_w1[1].astype(bf16), mlp_w2[1].astype(bf16), row(final_g))

    n_meta = meta_tokens.shape[0]
    n_sample, dec_seq, _ = x_sample.shape

    xm = meta_tokens.reshape(1, n_meta, d)
    xm1, conv_meta = _layer0(xm, jnp.zeros((1, CONV_HIST, d), jnp.float32), p0,
                             sb=1, tm=n_meta, name="meta_layer0")
    _, pool_meta = _layer1(xm1, jnp.zeros((1, POOL_HIST, d), jnp.float32), p1,
                           sb=1, tm=n_meta, pos0=0, name="meta_layer1")

    xp1, conv_prompt = _layer0(x_prompt, conv_meta, p0, sb=1, tm=PROMPT_TILE, shared_prev=True,
                               name="prompt_layer0")
    y_prompt, pool_prompt = _layer1(xp1, pool_meta, p1, sb=1, tm=PROMPT_TILE, pos0=n_meta,
                                    shared_prev=True, name="prompt_layer1")

    past_len = 16384
    xs1, conv_sample = _layer0(x_sample, state_conv[0], p0, sb=SAMPLE_SEQS_PER_TILE, tm=dec_seq,
                               name="sample_layer0")
    y_sample, pool_sample = _layer1(xs1, state_pool[0], p1, sb=SAMPLE_SEQS_PER_TILE, tm=dec_seq,
                                    pos0=past_len, name="sample_layer1")

    return (y_prompt, y_sample, conv_prompt[None], conv_sample[None], pool_prompt[None], pool_sample[None])
```

```python
import functools

import jax
import jax.numpy as jnp
from jax import lax
from jax.experimental import pallas as pl
from jax.experimental.pallas import tpu as pltpu

D_MODEL = 1024
D_FF = 4 * D_MODEL
CONV_W = 31
CONV_HIST = CONV_W - 1
POOL_WINDOWS = (2, 4, 8, 16)
GROUP_W = D_MODEL // len(POOL_WINDOWS)
POOL_HIST = max(POOL_WINDOWS) - 1
PAST_LEN = 16384
EPS = 1e-6

SUBLANES = 8
LANES = 128
N_COL = D_MODEL // LANES
ROW_BLK = 2 * SUBLANES
MXU_N = 256
CONV_HALO = -(-CONV_HIST // SUBLANES) * SUBLANES
POOL_HALO = -(-POOL_HIST // SUBLANES) * SUBLANES
V7X_VMEM_BYTES = 64 * 1024 * 1024


def _rmsnorm(x, g):
    y = x * lax.rsqrt(jnp.mean(x * x, axis=-1, keepdims=True) + EPS)
    return y * g


def _layernorm(x, g, b):
    mu = jnp.mean(x, axis=-1, keepdims=True)
    xc = x - mu
    var = jnp.mean(xc * xc, axis=-1, keepdims=True)
    return xc * lax.rsqrt(var + EPS) * g + b


def _pieces(r0, tm):
    if tm >= ROW_BLK:
        return [(r0 // tm, r0 % tm, ROW_BLK)]
    return [((r0 + i) // tm, 0, tm) for i in range(0, ROW_BLK, tm)]


def _load_rows(ref, r0, tm):
    parts = [ref[b, pl.ds(t0, n), :] for b, t0, n in _pieces(r0, tm)]
    return parts[0] if len(parts) == 1 else jnp.concatenate(parts, axis=0)


def _store_rows(ref, r0, tm, val):
    i = 0
    for b, t0, n in _pieces(r0, tm):
        ref[b, pl.ds(t0, n), :] = val[i:i + n]
        i += n


def _norm_rows_to(dst_ref, load_block, g, rows):
    for r0 in range(0, rows, ROW_BLK):
        dst_ref[r0:r0 + ROW_BLK, :] = _rmsnorm(load_block(r0), g).astype(jnp.bfloat16)


def _mlp_delta(h_ref, a_ref, w1_ref, w2_ref):
    a = jnp.dot(h_ref[...], w1_ref[...], preferred_element_type=jnp.float32)
    a_ref[...] = jnp.square(jnp.maximum(a, 0.0)).astype(jnp.bfloat16)
    return jnp.dot(a_ref[...], w2_ref[...], preferred_element_type=jnp.float32)


def _front_pos(n_tiles, nt):
    return lax.rem(jnp.minimum(pl.program_id(0), n_tiles - 1), nt)


def _load_history(ext_ref, prev_ref, sb, halo, hist):
    ext_ref[:, :, 0:SUBLANES, :] = jnp.zeros((sb, N_COL, SUBLANES, LANES), jnp.float32)
    for j in range(N_COL):
        ext_ref[:, j, halo - hist:halo, :] = prev_ref[0, :, :, j * LANES:(j + 1) * LANES]


def _emit_history(ext_ref, new_ref, tm, halo, hist):
    for j in range(N_COL):
        new_ref[0, :, :, j * LANES:(j + 1) * LANES] = ext_ref[:, j, tm + halo - hist:tm + halo, :]
    ext_ref[:, :, 0:halo, :] = ext_ref[:, :, tm:tm + halo, :]


def _layer0_kernel(x_ref, prev_ref, g_mix_ref, w_pw1_ref, b_pw1_ref, w_dw_ref, b_dw_ref,
                   ln_g_ref, ln_b_ref, w_pw2_ref, b_pw2_ref, g_mlp_ref, w1_ref, w2_ref,
                   out_ref, newconv_ref, ext_ref, xs_ref, x1_ref, h_ref, h2_ref, z_ref, a_ref,
                   *, sb, tm, nt, n_tiles):
    t_f = _front_pos(n_tiles, nt)
    rows = sb * tm
    d = D_MODEL
    off = CONV_HALO - CONV_HIST

    @pl.when(pl.program_id(0) == 0)
    def _():
        xs_ref[...] = jnp.zeros_like(xs_ref)
        z_ref[...] = jnp.zeros_like(z_ref)

    @pl.when(t_f == 0)
    def _():
        _load_history(ext_ref, prev_ref, sb, CONV_HALO, CONV_HIST)

    y = jnp.dot(z_ref[...], w_pw2_ref[...], preferred_element_type=jnp.float32) + b_pw2_ref[...]
    x1_ref[...] = xs_ref[...] + y

    _norm_rows_to(h_ref, lambda r0: _load_rows(x_ref, r0, tm), g_mix_ref[...], rows)
    for n in range(d // MXU_N):
        ca, cg = n * MXU_N, d + n * MXU_N
        ua = jnp.dot(h_ref[...], w_pw1_ref[:, ca:ca + MXU_N], preferred_element_type=jnp.float32)
        ug = jnp.dot(h_ref[...], w_pw1_ref[:, cg:cg + MXU_N], preferred_element_type=jnp.float32)
        v = (ua + b_pw1_ref[:, ca:ca + MXU_N]) * jax.nn.sigmoid(ug + b_pw1_ref[:, cg:cg + MXU_N])
        for jj in range(MXU_N // LANES):
            j = n * (MXU_N // LANES) + jj
            ext_ref[:, j, CONV_HALO:CONV_HALO + tm, :] = v[:, jj * LANES:(jj + 1) * LANES].reshape(sb, tm, LANES)

    _norm_rows_to(h2_ref, lambda r0: x1_ref[r0:r0 + ROW_BLK, :], g_mlp_ref[...], rows)
    delta = _mlp_delta(h2_ref, a_ref, w1_ref, w2_ref)
    out_ref[...] = (x1_ref[...] + delta).reshape(sb, tm, d)

    for r0 in range(0, rows, ROW_BLK):
        cols = []
        for j in range(N_COL):
            parts = []
            for b, t0, n in _pieces(r0, tm):
                acc = [None, None]
                for k in range(CONV_W):
                    term = ext_ref[b, j, pl.ds(t0 + k + off, n), :] * w_dw_ref[k:k + 1, j * LANES:(j + 1) * LANES]
                    acc[k % 2] = term if acc[k % 2] is None else acc[k % 2] + term
                parts.append(acc[0] + acc[1])
            cols.append(parts[0] if len(parts) == 1 else jnp.concatenate(parts, axis=0))
        c = jnp.concatenate(cols, axis=1) + b_dw_ref[...]
        z = _layernorm(c, ln_g_ref[...], ln_b_ref[...])
        z_ref[r0:r0 + ROW_BLK, :] = (z * jax.nn.sigmoid(z)).astype(jnp.bfloat16)
    xs_ref[...] = x_ref[...].reshape(rows, d)

    _emit_history(ext_ref, newconv_ref, tm, CONV_HALO, CONV_HIST)


def _layer1_kernel(x_ref, prev_ref, g_mix_ref, w_grp_ref, b_grp_ref, scale_ref, g_mlp_ref,
                   w1_ref, w2_ref, g_final_ref, y_ref, newpool_ref, ext_ref, xs_ref, x1_ref, h2_ref, p_ref, a_ref,
                   *, sb, tm, nt, n_tiles, pos0):
    t_f = _front_pos(n_tiles, nt)
    rows = sb * tm
    d = D_MODEL

    @pl.when(pl.program_id(0) == 0)
    def _():
        xs_ref[...] = jnp.zeros_like(xs_ref)
        p_ref[...] = jnp.zeros_like(p_ref)

    @pl.when(t_f == 0)
    def _():
        _load_history(ext_ref, prev_ref, sb, POOL_HALO, POOL_HIST)

    for g in range(len(POOL_WINDOWS)):
        lo, hi = g * GROUP_W, (g + 1) * GROUP_W
        mix = jnp.dot(p_ref[:, lo:hi], w_grp_ref[g], preferred_element_type=jnp.float32)
        x1_ref[:, lo:hi] = xs_ref[:, lo:hi] + (mix + b_grp_ref[:, lo:hi]) * scale_ref[:, lo:hi]
    _norm_rows_to(h2_ref, lambda r0: x1_ref[r0:r0 + ROW_BLK, :], g_mlp_ref[...], rows)
    delta = _mlp_delta(h2_ref, a_ref, w1_ref, w2_ref)
    y_ref[...] = (x1_ref[...] + delta).reshape(sb, tm, d)
    for r0 in range(0, rows, ROW_BLK):
        _store_rows(y_ref, r0, tm, _rmsnorm(_load_rows(y_ref, r0, tm), g_final_ref[...]))

    for r0 in range(0, rows, ROW_BLK):
        h = _rmsnorm(_load_rows(x_ref, r0, tm), g_mix_ref[...])
        i = 0
        for b, t0, n in _pieces(r0, tm):
            for j in range(N_COL):
                ext_ref[b, j, pl.ds(POOL_HALO + t0, n), :] = h[i:i + n, j * LANES:(j + 1) * LANES]
            i += n
    for r0 in range(0, rows, ROW_BLK):
        cols = []
        for j in range(N_COL):
            w = POOL_WINDOWS[j * LANES // GROUP_W]
            parts = []
            for b, t0, n in _pieces(r0, tm):
                cur = ext_ref[b, j, pl.ds(POOL_HALO + t0, n), :]
                s = cur
                for i in range(1, w):
                    s = s + ext_ref[b, j, pl.ds(POOL_HALO + t0 - i, n), :]
                pos = pos0 + t_f * tm + t0 + lax.broadcasted_iota(jnp.int32, (n, 1), 0)
                inv_cnt = 1.0 / jnp.minimum(w, pos + 1).astype(jnp.float32)
                parts.append(s * inv_cnt - cur)
            cols.append(parts[0] if len(parts) == 1 else jnp.concatenate(parts, axis=0))
        p_ref[r0:r0 + ROW_BLK, :] = jnp.concatenate(cols, axis=1).astype(jnp.bfloat16)
    xs_ref[...] = x_ref[...].reshape(rows, d)

    _emit_history(ext_ref, newpool_ref, tm, POOL_HALO, POOL_HIST)


def _resident(p, layer):
    if layer is None:
        zeros = (0,) * p.ndim
        return pl.BlockSpec(p.shape, lambda s: zeros, pipeline_mode=pl.Buffered(1))
    index = (layer,) + (0,) * (p.ndim - 1)
    return pl.BlockSpec((None,) + p.shape[1:], lambda s: index, pipeline_mode=pl.Buffered(1))


def _run_layer(body, x, prev, params, *, hist, halo, extra_scratch, sb, tm, name, **static):
    nb, seq, d = x.shape
    rows = sb * tm
    assert nb % sb == 0 and seq % tm == 0 and rows % ROW_BLK == 0
    assert tm % ROW_BLK == 0 or ROW_BLK % tm == 0 and tm % SUBLANES == 0
    shared_prev = prev.shape[1] != nb
    assert prev.shape == (1, 1 if shared_prev else nb, hist, d) and (not shared_prev or sb == 1)
    nt = seq // tm
    n_tiles = (nb // sb) * nt

    def front(s):
        tau = jnp.minimum(s, n_tiles - 1)
        return tau // nt, tau % nt

    def back(s):
        tau = jnp.maximum(s - 1, 0)
        return tau // nt, tau % nt

    x_spec = pl.BlockSpec((sb, tm, d), lambda s: (*front(s), 0))
    out_spec = pl.BlockSpec((sb, tm, d), lambda s: (*back(s), 0))
    prev_spec = pl.BlockSpec((1, sb, hist, d),
                             (lambda s: (0, 0, 0, 0)) if shared_prev else (lambda s: (0, front(s)[0], 0, 0)))
    state_spec = pl.BlockSpec((1, sb, hist, d), lambda s: (0, front(s)[0], 0, 0))

    scratch = [pltpu.VMEM((sb, N_COL, halo + tm, LANES), jnp.float32),
               pltpu.VMEM((rows, d), jnp.float32),
               pltpu.VMEM((rows, d), jnp.float32)]
    scratch += [pltpu.VMEM((rows, width), jnp.bfloat16) for width in extra_scratch]

    resident = sum((p.size if layer is None else p.size // p.shape[0]) * p.dtype.itemsize for p, layer in params)
    tile = rows * d * 4
    pipelined = 2 * (2 * tile + 2 * sb * (-(-hist // SUBLANES) * SUBLANES) * d * 4)
    scratch_bytes = sb * (halo + tm) * d * 4 + 2 * tile + sum(rows * width * 2 for width in extra_scratch)
    temporaries = rows * D_FF * 4
    vmem_limit = min(resident + pipelined + scratch_bytes + temporaries, V7X_VMEM_BYTES - (4 << 20))

    return pl.pallas_call(
        functools.partial(body, sb=sb, tm=tm, nt=nt, n_tiles=n_tiles, **static),
        grid=(n_tiles + 1,),
        in_specs=[x_spec, prev_spec] + [_resident(p, layer) for p, layer in params],
        out_specs=(out_spec, state_spec),
        out_shape=(jax.ShapeDtypeStruct((nb, seq, d), jnp.float32),
                   jax.ShapeDtypeStruct((1, nb, hist, d), jnp.float32)),
        scratch_shapes=scratch,
        compiler_params=pltpu.CompilerParams(dimension_semantics=("arbitrary",),
                                             vmem_limit_bytes=int(vmem_limit)),
        name=name,
    )(x, prev, *[p for p, _ in params])


def _layer0(x, prev, params, *, sb, tm, name):
    return _run_layer(_layer0_kernel, x, prev, params, hist=CONV_HIST, halo=CONV_HALO,
                      extra_scratch=(D_MODEL, D_MODEL, D_MODEL, D_FF), sb=sb, tm=tm, name=name)


def _layer1(x, prev, params, *, sb, tm, pos0, name):
    return _run_layer(_layer1_kernel, x, prev, params, hist=POOL_HIST, halo=POOL_HALO,
                      extra_scratch=(D_MODEL, D_MODEL, D_FF), sb=sb, tm=tm, name=name, pos0=pos0)


PROMPT_TILE = 512
SAMPLE_SEQS_PER_TILE = 16


def kernel(x_prompt, x_sample, state_conv, state_pool, meta_tokens, norm_mix_g, norm_mlp_g, conv_w_pw1, conv_b_pw1, conv_w_dw, conv_b_dw, conv_ln_g, conv_ln_b, conv_w_pw2, conv_b_pw2, pool_w, pool_b, pool_scale, mlp_w1, mlp_w2, final_g):
    assert norm_mix_g.shape[0] == 2 and conv_w_pw1.shape[0] == 1 and pool_w.shape[0] == 1
    bf16 = jnp.bfloat16
    d = D_MODEL
    row = lambda a: (a.reshape(1, -1), None)
    w1, w2 = mlp_w1.astype(bf16), mlp_w2.astype(bf16)
    p0 = [row(norm_mix_g[0]), (conv_w_pw1.astype(bf16), 0), row(conv_b_pw1[0]), (conv_w_dw, 0),
          row(conv_b_dw[0]), row(conv_ln_g[0]), row(conv_ln_b[0]), (conv_w_pw2.astype(bf16), 0),
          row(conv_b_pw2[0]), row(norm_mlp_g[0]), (w1, 0), (w2, 0)]
    p1 = [row(norm_mix_g[1]), (pool_w.astype(bf16), 0), row(pool_b[0]), row(pool_scale[0]),
          row(norm_mlp_g[1]), (w1, 1), (w2, 1), row(final_g)]

    n_meta = meta_tokens.shape[0]
    dec_seq = x_sample.shape[1]

    xm = meta_tokens.reshape(1, n_meta, d)
    xm1, conv_meta = _layer0(xm, jnp.zeros((1, 1, CONV_HIST, d), jnp.float32), p0,
                             sb=1, tm=n_meta, name="meta_layer0")
    _, pool_meta = _layer1(xm1, jnp.zeros((1, 1, POOL_HIST, d), jnp.float32), p1,
                           sb=1, tm=n_meta, pos0=0, name="meta_layer1")

    xp1, conv_prompt = _layer0(x_prompt, conv_meta, p0, sb=1, tm=PROMPT_TILE, name="prompt_layer0")
    y_prompt, pool_prompt = _layer1(xp1, pool_meta, p1, sb=1, tm=PROMPT_TILE, pos0=n_meta,
                                    name="prompt_layer1")

    xs1, conv_sample = _layer0(x_sample, state_conv, p0, sb=SAMPLE_SEQS_PER_TILE, tm=dec_seq,
                               name="sample_layer0")
    y_sample, pool_sample = _layer1(xs1, state_pool, p1, sb=SAMPLE_SEQS_PER_TILE, tm=dec_seq,
                                    pos0=PAST_LEN, name="sample_layer1")

    return (y_prompt, y_sample, conv_prompt, conv_sample, pool_prompt, pool_sample)
```

```python
import functools

import jax
import jax.numpy as jnp
from jax import lax
from jax.experimental import pallas as pl
from jax.experimental.pallas import tpu as pltpu

D_MODEL = 1024
D_FF = 4 * D_MODEL
CONV_W = 31
CONV_HIST = CONV_W - 1
POOL_WINDOWS = (2, 4, 8, 16)
GROUP_W = D_MODEL // len(POOL_WINDOWS)
POOL_HIST = max(POOL_WINDOWS) - 1
PAST_LEN = 16384
EPS = 1e-6

SUBLANES = 8
LANES = 128
N_COL = D_MODEL // LANES
ROW_BLK = 2 * SUBLANES
MXU_N = 256
CONV_HALO = -(-CONV_HIST // SUBLANES) * SUBLANES
POOL_HALO = -(-POOL_HIST // SUBLANES) * SUBLANES
V7X_VMEM_BYTES = 64 * 1024 * 1024


def _rmsnorm(x, g):
    y = x * lax.rsqrt(jnp.mean(x * x, axis=-1, keepdims=True) + EPS)
    return y * g


def _layernorm(x, g, b):
    mu = jnp.mean(x, axis=-1, keepdims=True)
    xc = x - mu
    var = jnp.mean(xc * xc, axis=-1, keepdims=True)
    return xc * lax.rsqrt(var + EPS) * g + b


def _pieces(r0, tm):
    if tm >= ROW_BLK:
        return [(r0 // tm, r0 % tm, ROW_BLK)]
    return [((r0 + i) // tm, 0, tm) for i in range(0, ROW_BLK, tm)]


def _load_rows(ref, r0, tm):
    parts = [ref[b, pl.ds(t0, n), :] for b, t0, n in _pieces(r0, tm)]
    return parts[0] if len(parts) == 1 else jnp.concatenate(parts, axis=0)


def _store_rows(ref, r0, tm, val):
    i = 0
    for b, t0, n in _pieces(r0, tm):
        ref[b, pl.ds(t0, n), :] = val[i:i + n]
        i += n


def _norm_rows_to(dst_ref, load_block, g, rows):
    for r0 in range(0, rows, ROW_BLK):
        dst_ref[r0:r0 + ROW_BLK, :] = _rmsnorm(load_block(r0), g).astype(jnp.bfloat16)


def _mlp_delta(h_ref, a_ref, w1_ref, w2_ref):
    a = jnp.dot(h_ref[...], w1_ref[...], preferred_element_type=jnp.float32)
    a_ref[...] = jnp.square(jnp.maximum(a, 0.0)).astype(jnp.bfloat16)
    return jnp.dot(a_ref[...], w2_ref[...], preferred_element_type=jnp.float32)


def _ordered_zero(v):
    bits = pltpu.bitcast(v[0:SUBLANES] + v[SUBLANES:ROW_BLK], jnp.uint32)
    zero = pltpu.bitcast((bits >> 16) >> 16, jnp.float32)
    return jnp.concatenate([zero] * (ROW_BLK // SUBLANES), axis=0)


def _front_pos(n_tiles, nt):
    return lax.rem(jnp.minimum(pl.program_id(0), n_tiles - 1), nt)


def _load_history(ext_ref, prev_ref, sb, halo, hist):
    ext_ref[:, :, 0:SUBLANES, :] = jnp.zeros((sb, N_COL, SUBLANES, LANES), jnp.float32)
    for j in range(N_COL):
        ext_ref[:, j, halo - hist:halo, :] = prev_ref[0, :, :, j * LANES:(j + 1) * LANES]


def _emit_history(ext_ref, new_ref, tm, halo, hist):
    for j in range(N_COL):
        new_ref[0, :, :, j * LANES:(j + 1) * LANES] = ext_ref[:, j, tm + halo - hist:tm + halo, :]
    ext_ref[:, :, 0:halo, :] = ext_ref[:, :, tm:tm + halo, :]


def _layer0_kernel(x_ref, prev_ref, g_mix_ref, w_pw1_ref, b_pw1_ref, w_dw_ref, b_dw_ref,
                   ln_g_ref, ln_b_ref, w_pw2_ref, b_pw2_ref, g_mlp_ref, w1_ref, w2_ref,
                   out_ref, newconv_ref, ext_ref, xs_ref, x1_ref, h_ref, h2_ref, z_ref, a_ref,
                   *, sb, tm, nt, n_tiles):
    t_f = _front_pos(n_tiles, nt)
    rows = sb * tm
    d = D_MODEL
    off = CONV_HALO - CONV_HIST

    @pl.when(pl.program_id(0) == 0)
    def _():
        xs_ref[...] = jnp.zeros_like(xs_ref)
        z_ref[...] = jnp.zeros_like(z_ref)

    @pl.when(t_f == 0)
    def _():
        _load_history(ext_ref, prev_ref, sb, CONV_HALO, CONV_HIST)

    y = jnp.dot(z_ref[...], w_pw2_ref[...], preferred_element_type=jnp.float32) + b_pw2_ref[...]
    x1_ref[...] = xs_ref[...] + y

    _norm_rows_to(h_ref, lambda r0: _load_rows(x_ref, r0, tm), g_mix_ref[...], rows)
    for n in range(d // MXU_N):
        ca, cg = n * MXU_N, d + n * MXU_N
        ua = jnp.dot(h_ref[...], w_pw1_ref[:, ca:ca + MXU_N], preferred_element_type=jnp.float32)
        ug = jnp.dot(h_ref[...], w_pw1_ref[:, cg:cg + MXU_N], preferred_element_type=jnp.float32)
        v = (ua + b_pw1_ref[:, ca:ca + MXU_N]) * jax.nn.sigmoid(ug + b_pw1_ref[:, cg:cg + MXU_N])
        for jj in range(MXU_N // LANES):
            j = n * (MXU_N // LANES) + jj
            ext_ref[:, j, CONV_HALO:CONV_HALO + tm, :] = v[:, jj * LANES:(jj + 1) * LANES].reshape(sb, tm, LANES)

    _norm_rows_to(h2_ref, lambda r0: x1_ref[r0:r0 + ROW_BLK, :], g_mlp_ref[...], rows)
    delta = _mlp_delta(h2_ref, a_ref, w1_ref, w2_ref)
    out_ref[...] = (x1_ref[...] + delta).reshape(sb, tm, d)

    chain = None
    for r0 in range(0, rows, ROW_BLK):
        cols = []
        for j in range(N_COL):
            parts = []
            for b, t0, n in _pieces(r0, tm):
                acc = [None if chain is None else chain[0:n], None]
                for k in range(CONV_W):
                    term = ext_ref[b, j, pl.ds(t0 + k + off, n), :] * w_dw_ref[k:k + 1, j * LANES:(j + 1) * LANES]
                    acc[k % 2] = term if acc[k % 2] is None else acc[k % 2] + term
                parts.append(acc[0] + acc[1])
            cols.append(parts[0] if len(parts) == 1 else jnp.concatenate(parts, axis=0))
        c = jnp.concatenate(cols, axis=1) + b_dw_ref[...]
        z = _layernorm(c, ln_g_ref[...], ln_b_ref[...])
        z = z * jax.nn.sigmoid(z)
        z_ref[r0:r0 + ROW_BLK, :] = z.astype(jnp.bfloat16)
        chain = _ordered_zero(z[:, 0:LANES])
    xs_ref[...] = x_ref[...].reshape(rows, d)

    _emit_history(ext_ref, newconv_ref, tm, CONV_HALO, CONV_HIST)


def _layer1_kernel(x_ref, prev_ref, g_mix_ref, w_grp_ref, b_grp_ref, scale_ref, g_mlp_ref,
                   w1_ref, w2_ref, g_final_ref, y_ref, newpool_ref, ext_ref, xs_ref, x1_ref, h2_ref, p_ref, a_ref,
                   *, sb, tm, nt, n_tiles, pos0):
    t_f = _front_pos(n_tiles, nt)
    rows = sb * tm
    d = D_MODEL

    @pl.when(pl.program_id(0) == 0)
    def _():
        xs_ref[...] = jnp.zeros_like(xs_ref)
        p_ref[...] = jnp.zeros_like(p_ref)

    @pl.when(t_f == 0)
    def _():
        _load_history(ext_ref, prev_ref, sb, POOL_HALO, POOL_HIST)

    for g in range(len(POOL_WINDOWS)):
        lo, hi = g * GROUP_W, (g + 1) * GROUP_W
        mix = jnp.dot(p_ref[:, lo:hi], w_grp_ref[g], preferred_element_type=jnp.float32)
        x1_ref[:, lo:hi] = xs_ref[:, lo:hi] + (mix + b_grp_ref[:, lo:hi]) * scale_ref[:, lo:hi]
    _norm_rows_to(h2_ref, lambda r0: x1_ref[r0:r0 + ROW_BLK, :], g_mlp_ref[...], rows)
    delta = _mlp_delta(h2_ref, a_ref, w1_ref, w2_ref)
    y_ref[...] = (x1_ref[...] + delta).reshape(sb, tm, d)
    for r0 in range(0, rows, ROW_BLK):
        _store_rows(y_ref, r0, tm, _rmsnorm(_load_rows(y_ref, r0, tm), g_final_ref[...]))

    chain = None
    for r0 in range(0, rows, ROW_BLK):
        xb = _load_rows(x_ref, r0, tm)
        if chain is not None:
            xb = xb + jnp.concatenate([chain] * N_COL, axis=1)
        h = _rmsnorm(xb, g_mix_ref[...])
        i = 0
        for b, t0, n in _pieces(r0, tm):
            for j in range(N_COL):
                ext_ref[b, j, pl.ds(POOL_HALO + t0, n), :] = h[i:i + n, j * LANES:(j + 1) * LANES]
            i += n
        cols = []
        for j in range(N_COL):
            w = POOL_WINDOWS[j * LANES // GROUP_W]
            parts = []
            for b, t0, n in _pieces(r0, tm):
                cur = ext_ref[b, j, pl.ds(POOL_HALO + t0, n), :]
                s = cur
                for i in range(1, w):
                    s = s + ext_ref[b, j, pl.ds(POOL_HALO + t0 - i, n), :]
                pos = pos0 + t_f * tm + t0 + lax.broadcasted_iota(jnp.int32, (n, 1), 0)
                inv_cnt = 1.0 / jnp.minimum(w, pos + 1).astype(jnp.float32)
                parts.append(s * inv_cnt - cur)
            cols.append(parts[0] if len(parts) == 1 else jnp.concatenate(parts, axis=0))
        p_ref[r0:r0 + ROW_BLK, :] = jnp.concatenate(cols, axis=1).astype(jnp.bfloat16)
        chain = _ordered_zero(functools.reduce(lambda a, b: a + b, cols))
    xs_ref[...] = x_ref[...].reshape(rows, d)

    _emit_history(ext_ref, newpool_ref, tm, POOL_HALO, POOL_HIST)


def _resident(p, layer):
    if layer is None:
        zeros = (0,) * p.ndim
        return pl.BlockSpec(p.shape, lambda s: zeros, pipeline_mode=pl.Buffered(1))
    index = (layer,) + (0,) * (p.ndim - 1)
    return pl.BlockSpec((None,) + p.shape[1:], lambda s: index, pipeline_mode=pl.Buffered(1))


def _run_layer(body, x, prev, params, *, hist, halo, extra_scratch, sb, tm, name, **static):
    nb, seq, d = x.shape
    rows = sb * tm
    assert nb % sb == 0 and seq % tm == 0 and rows % ROW_BLK == 0
    assert tm % ROW_BLK == 0 or ROW_BLK % tm == 0 and tm % SUBLANES == 0
    shared_prev = prev.shape[1] != nb
    assert prev.shape == (1, 1 if shared_prev else nb, hist, d) and (not shared_prev or sb == 1)
    nt = seq // tm
    n_tiles = (nb // sb) * nt

    def front(s):
        tau = jnp.minimum(s, n_tiles - 1)
        return tau // nt, tau % nt

    def back(s):
        tau = jnp.maximum(s - 1, 0)
        return tau // nt, tau % nt

    x_spec = pl.BlockSpec((sb, tm, d), lambda s: (*front(s), 0))
    out_spec = pl.BlockSpec((sb, tm, d), lambda s: (*back(s), 0))
    prev_spec = pl.BlockSpec((1, sb, hist, d),
                             (lambda s: (0, 0, 0, 0)) if shared_prev else (lambda s: (0, front(s)[0], 0, 0)))
    state_spec = pl.BlockSpec((1, sb, hist, d), lambda s: (0, front(s)[0], 0, 0))

    scratch = [pltpu.VMEM((sb, N_COL, halo + tm, LANES), jnp.float32),
               pltpu.VMEM((rows, d), jnp.float32),
               pltpu.VMEM((rows, d), jnp.float32)]
    scratch += [pltpu.VMEM((rows, width), jnp.bfloat16) for width in extra_scratch]

    resident = sum((p.size if layer is None else p.size // p.shape[0]) * p.dtype.itemsize for p, layer in params)
    tile = rows * d * 4
    pipelined = 2 * (2 * tile + 2 * sb * (-(-hist // SUBLANES) * SUBLANES) * d * 4)
    scratch_bytes = sb * (halo + tm) * d * 4 + 2 * tile + sum(rows * width * 2 for width in extra_scratch)
    temporaries = rows * D_FF * 4
    vmem_limit = min(resident + pipelined + scratch_bytes + temporaries, V7X_VMEM_BYTES - (4 << 20))

    return pl.pallas_call(
        functools.partial(body, sb=sb, tm=tm, nt=nt, n_tiles=n_tiles, **static),
        grid=(n_tiles + 1,),
        in_specs=[x_spec, prev_spec] + [_resident(p, layer) for p, layer in params],
        out_specs=(out_spec, state_spec),
        out_shape=(jax.ShapeDtypeStruct((nb, seq, d), jnp.float32),
                   jax.ShapeDtypeStruct((1, nb, hist, d), jnp.float32)),
        scratch_shapes=scratch,
        compiler_params=pltpu.CompilerParams(dimension_semantics=("arbitrary",),
                                             vmem_limit_bytes=int(vmem_limit)),
        name=name,
    )(x, prev, *[p for p, _ in params])


def _layer0(x, prev, params, *, sb, tm, name):
    return _run_layer(_layer0_kernel, x, prev, params, hist=CONV_HIST, halo=CONV_HALO,
                      extra_scratch=(D_MODEL, D_MODEL, D_MODEL, D_FF), sb=sb, tm=tm, name=name)


def _layer1(x, prev, params, *, sb, tm, pos0, name):
    return _run_layer(_layer1_kernel, x, prev, params, hist=POOL_HIST, halo=POOL_HALO,
                      extra_scratch=(D_MODEL, D_MODEL, D_FF), sb=sb, tm=tm, name=name, pos0=pos0)


PROMPT_TILE = 512
SAMPLE_SEQS_PER_TILE = 32


def kernel(x_prompt, x_sample, state_conv, state_pool, meta_tokens, norm_mix_g, norm_mlp_g, conv_w_pw1, conv_b_pw1, conv_w_dw, conv_b_dw, conv_ln_g, conv_ln_b, conv_w_pw2, conv_b_pw2, pool_w, pool_b, pool_scale, mlp_w1, mlp_w2, final_g):
    assert norm_mix_g.shape[0] == 2 and conv_w_pw1.shape[0] == 1 and pool_w.shape[0] == 1
    bf16 = jnp.bfloat16
    d = D_MODEL
    row = lambda a: (a.reshape(1, -1), None)
    w1, w2 = mlp_w1.astype(bf16), mlp_w2.astype(bf16)
    p0 = [row(norm_mix_g[0]), (conv_w_pw1.astype(bf16), 0), row(conv_b_pw1[0]), (conv_w_dw, 0),
          row(conv_b_dw[0]), row(conv_ln_g[0]), row(conv_ln_b[0]), (conv_w_pw2.astype(bf16), 0),
          row(conv_b_pw2[0]), row(norm_mlp_g[0]), (w1, 0), (w2, 0)]
    p1 = [row(norm_mix_g[1]), (pool_w.astype(bf16), 0), row(pool_b[0]), row(pool_scale[0]),
          row(norm_mlp_g[1]), (w1, 1), (w2, 1), row(final_g)]

    n_meta = meta_tokens.shape[0]
    dec_seq = x_sample.shape[1]

    xm = meta_tokens.reshape(1, n_meta, d)
    xm1, conv_meta = _layer0(xm, jnp.zeros((1, 1, CONV_HIST, d), jnp.float32), p0,
                             sb=1, tm=n_meta, name="meta_layer0")
    _, pool_meta = _layer1(xm1, jnp.zeros((1, 1, POOL_HIST, d), jnp.float32), p1,
                           sb=1, tm=n_meta, pos0=0, name="meta_layer1")

    xp1, conv_prompt = _layer0(x_prompt, conv_meta, p0, sb=1, tm=PROMPT_TILE, name="prompt_layer0")
    y_prompt, pool_prompt = _layer1(xp1, pool_meta, p1, sb=1, tm=PROMPT_TILE, pos0=n_meta,
                                    name="prompt_layer1")

    xs1, conv_sample = _layer0(x_sample, state_conv, p0, sb=SAMPLE_SEQS_PER_TILE, tm=dec_seq,
                               name="sample_layer0")
    y_sample, pool_sample = _layer1(xs1, state_pool, p1, sb=SAMPLE_SEQS_PER_TILE, tm=dec_seq,
                                    pos0=PAST_LEN, name="sample_layer1")

    return (y_prompt, y_sample, conv_prompt, conv_sample, pool_prompt, pool_sample)
```

```python
import functools

import jax
import jax.numpy as jnp
from jax import lax
from jax.experimental import pallas as pl
from jax.experimental.pallas import tpu as pltpu

D_MODEL = 1024
D_FF = 4 * D_MODEL
CONV_W = 31
CONV_HIST = CONV_W - 1
POOL_WINDOWS = (2, 4, 8, 16)
GROUP_W = D_MODEL // len(POOL_WINDOWS)
POOL_HIST = max(POOL_WINDOWS) - 1
PAST_LEN = 16384
EPS = 1e-6

SUBLANES = 8
LANES = 128
N_COL = D_MODEL // LANES
ROW_BLK = 2 * SUBLANES
MXU_N = 256
V7X_VMEM_BYTES = 64 * 1024 * 1024


def _rmsnorm(x, g):
    y = x * lax.rsqrt(jnp.mean(x * x, axis=-1, keepdims=True) + EPS)
    return y * g


def _layernorm(x, g, b):
    mu = jnp.mean(x, axis=-1, keepdims=True)
    xc = x - mu
    var = jnp.mean(xc * xc, axis=-1, keepdims=True)
    return xc * lax.rsqrt(var + EPS) * g + b


def _col(j):
    return slice(j * LANES, (j + 1) * LANES)


class _SeqTile:
    def __init__(self, tm, hist):
        assert tm % ROW_BLK == 0
        self.tm, self.hist, self.rows = tm, hist, tm
        self.halo = -(-hist // SUBLANES) * SUBLANES
        self.x_block = (1, tm, D_MODEL)
        self.state_block = (1, 1, hist, D_MODEL)
        self.ext_shape = (N_COL, self.halo + tm, LANES)

    def x_index(self, tile, nt):
        return (tile // nt, tile % nt, 0)

    def state_index(self, tile, nt):
        return (0, tile // nt, 0, 0)

    def state_shape(self, n_seq_tiles):
        return (1, n_seq_tiles, self.hist, D_MODEL)

    def load_rows(self, ref, r0):
        return ref[0, pl.ds(r0, ROW_BLK), :]

    def store_rows(self, ref, r0, val):
        ref[0, pl.ds(r0, ROW_BLK), :] = val

    def load_history(self, ext_ref, prev_ref):
        ext_ref[:, 0:SUBLANES, :] = jnp.zeros((N_COL, SUBLANES, LANES), jnp.float32)
        for j in range(N_COL):
            ext_ref[j, self.halo - self.hist:self.halo, :] = prev_ref[0, 0, :, _col(j)]

    def emit_history(self, ext_ref, new_ref):
        for j in range(N_COL):
            new_ref[0, 0, :, _col(j)] = ext_ref[j, self.tm + self.halo - self.hist:self.tm + self.halo, :]
        ext_ref[:, 0:self.halo, :] = ext_ref[:, self.tm:self.tm + self.halo, :]

    def store_tile_col(self, ext_ref, j, col):
        ext_ref[j, self.halo:self.halo + self.tm, :] = col

    def store_block_col(self, ext_ref, j, r0, col):
        ext_ref[j, pl.ds(self.halo + r0, ROW_BLK), :] = col

    def window(self, ext_ref, j, r0, back):
        return ext_ref[j, pl.ds(self.halo + r0 - back, ROW_BLK), :]

    def positions(self, r0, t_f):
        return t_f * self.tm + r0 + lax.broadcasted_iota(jnp.int32, (ROW_BLK, 1), 0)


class _TimeMajorTile:
    def __init__(self, sb, tm, hist):
        assert sb % ROW_BLK == 0
        self.sb, self.tm, self.hist, self.rows = sb, tm, hist, sb * tm
        self.x_block = (tm, sb, D_MODEL)
        self.state_block = (1, hist, sb, D_MODEL)
        self.ext_shape = (N_COL, hist + tm, sb, LANES)

    def x_index(self, tile, nt):
        return (0, tile, 0)

    def state_index(self, tile, nt):
        return (0, 0, tile, 0)

    def state_shape(self, n_seq_tiles):
        return (1, self.hist, n_seq_tiles * self.sb, D_MODEL)

    def load_rows(self, ref, r0):
        return ref[r0 // self.sb, pl.ds(r0 % self.sb, ROW_BLK), :]

    def store_rows(self, ref, r0, val):
        ref[r0 // self.sb, pl.ds(r0 % self.sb, ROW_BLK), :] = val

    def load_history(self, ext_ref, prev_ref):
        for j in range(N_COL):
            ext_ref[j, 0:self.hist, :, :] = prev_ref[0, :, :, _col(j)]

    def emit_history(self, ext_ref, new_ref):
        for j in range(N_COL):
            new_ref[0, :, :, _col(j)] = ext_ref[j, self.tm:self.tm + self.hist, :, :]

    def store_tile_col(self, ext_ref, j, col):
        ext_ref[j, self.hist:self.hist + self.tm, :, :] = col.reshape(self.tm, self.sb, LANES)

    def store_block_col(self, ext_ref, j, r0, col):
        ext_ref[j, self.hist + r0 // self.sb, pl.ds(r0 % self.sb, ROW_BLK), :] = col

    def window(self, ext_ref, j, r0, back):
        return ext_ref[j, self.hist + r0 // self.sb - back, pl.ds(r0 % self.sb, ROW_BLK), :]

    def positions(self, r0, t_f):
        return jnp.full((ROW_BLK, 1), r0 // self.sb, jnp.int32)


def _norm_rows_to(dst_ref, load_block, g, rows):
    for r0 in range(0, rows, ROW_BLK):
        dst_ref[r0:r0 + ROW_BLK, :] = _rmsnorm(load_block(r0), g).astype(jnp.bfloat16)


def _mlp_delta(h_ref, a_ref, w1_ref, w2_ref):
    a = jnp.dot(h_ref[...], w1_ref[...], preferred_element_type=jnp.float32)
    a_ref[...] = jnp.square(jnp.maximum(a, 0.0)).astype(jnp.bfloat16)
    return jnp.dot(a_ref[...], w2_ref[...], preferred_element_type=jnp.float32)


def _ordered_zero(v):
    bits = pltpu.bitcast(v[0:SUBLANES] + v[SUBLANES:ROW_BLK], jnp.uint32)
    zero = pltpu.bitcast((bits >> 16) >> 16, jnp.float32)
    return jnp.concatenate([zero] * (ROW_BLK // SUBLANES), axis=0)


def _front_pos(n_tiles, nt):
    return lax.rem(jnp.minimum(pl.program_id(0), n_tiles - 1), nt)


def _layer0_kernel(x_ref, prev_ref, g_mix_ref, w_pw1_ref, b_pw1_ref, w_dw_ref, b_dw_ref,
                   ln_g_ref, ln_b_ref, w_pw2_ref, b_pw2_ref, g_mlp_ref, w1_ref, w2_ref,
                   out_ref, newconv_ref, ext_ref, xs_ref, x1_ref, h_ref, h2_ref, z_ref, a_ref,
                   *, geo, nt, n_tiles):
    t_f = _front_pos(n_tiles, nt)
    rows = geo.rows
    d = D_MODEL

    @pl.when(pl.program_id(0) == 0)
    def _():
        xs_ref[...] = jnp.zeros_like(xs_ref)
        z_ref[...] = jnp.zeros_like(z_ref)

    @pl.when(t_f == 0)
    def _():
        geo.load_history(ext_ref, prev_ref)

    y = jnp.dot(z_ref[...], w_pw2_ref[...], preferred_element_type=jnp.float32) + b_pw2_ref[...]
    x1_ref[...] = xs_ref[...] + y

    _norm_rows_to(h_ref, lambda r0: geo.load_rows(x_ref, r0), g_mix_ref[...], rows)
    for n in range(d // MXU_N):
        ca, cg = n * MXU_N, d + n * MXU_N
        ua = jnp.dot(h_ref[...], w_pw1_ref[:, ca:ca + MXU_N], preferred_element_type=jnp.float32)
        ug = jnp.dot(h_ref[...], w_pw1_ref[:, cg:cg + MXU_N], preferred_element_type=jnp.float32)
        v = (ua + b_pw1_ref[:, ca:ca + MXU_N]) * jax.nn.sigmoid(ug + b_pw1_ref[:, cg:cg + MXU_N])
        for jj in range(MXU_N // LANES):
            geo.store_tile_col(ext_ref, n * (MXU_N // LANES) + jj, v[:, _col(jj)])

    _norm_rows_to(h2_ref, lambda r0: x1_ref[r0:r0 + ROW_BLK, :], g_mlp_ref[...], rows)
    delta = _mlp_delta(h2_ref, a_ref, w1_ref, w2_ref)
    out_ref[...] = (x1_ref[...] + delta).reshape(out_ref.shape)

    chain = None
    for r0 in range(0, rows, ROW_BLK):
        cols = []
        for j in range(N_COL):
            acc = [chain, None]
            for k in range(CONV_W):
                term = geo.window(ext_ref, j, r0, CONV_HIST - k) * w_dw_ref[k:k + 1, _col(j)]
                acc[k % 2] = term if acc[k % 2] is None else acc[k % 2] + term
            cols.append(acc[0] + acc[1])
        c = jnp.concatenate(cols, axis=1) + b_dw_ref[...]
        z = _layernorm(c, ln_g_ref[...], ln_b_ref[...])
        z = z * jax.nn.sigmoid(z)
        z_ref[r0:r0 + ROW_BLK, :] = z.astype(jnp.bfloat16)
        chain = _ordered_zero(z[:, 0:LANES])
    xs_ref[...] = x_ref[...].reshape(rows, d)

    geo.emit_history(ext_ref, newconv_ref)


def _layer1_kernel(x_ref, prev_ref, g_mix_ref, w_grp_ref, b_grp_ref, scale_ref, g_mlp_ref,
                   w1_ref, w2_ref, g_final_ref, y_ref, newpool_ref, ext_ref, xs_ref, x1_ref, h2_ref, p_ref, a_ref,
                   *, geo, nt, n_tiles, pos0):
    t_f = _front_pos(n_tiles, nt)
    rows = geo.rows
    d = D_MODEL

    @pl.when(pl.program_id(0) == 0)
    def _():
        xs_ref[...] = jnp.zeros_like(xs_ref)
        p_ref[...] = jnp.zeros_like(p_ref)

    @pl.when(t_f == 0)
    def _():
        geo.load_history(ext_ref, prev_ref)

    for g in range(len(POOL_WINDOWS)):
        lo, hi = g * GROUP_W, (g + 1) * GROUP_W
        mix = jnp.dot(p_ref[:, lo:hi], w_grp_ref[g], preferred_element_type=jnp.float32)
        x1_ref[:, lo:hi] = xs_ref[:, lo:hi] + (mix + b_grp_ref[:, lo:hi]) * scale_ref[:, lo:hi]
    _norm_rows_to(h2_ref, lambda r0: x1_ref[r0:r0 + ROW_BLK, :], g_mlp_ref[...], rows)
    delta = _mlp_delta(h2_ref, a_ref, w1_ref, w2_ref)
    y_ref[...] = (x1_ref[...] + delta).reshape(y_ref.shape)
    for r0 in range(0, rows, ROW_BLK):
        geo.store_rows(y_ref, r0, _rmsnorm(geo.load_rows(y_ref, r0), g_final_ref[...]))

    chain = None
    for r0 in range(0, rows, ROW_BLK):
        xb = geo.load_rows(x_ref, r0)
        if chain is not None:
            xb = xb + jnp.concatenate([chain] * N_COL, axis=1)
        h = _rmsnorm(xb, g_mix_ref[...])
        for j in range(N_COL):
            geo.store_block_col(ext_ref, j, r0, h[:, _col(j)])
        pos = pos0 + geo.positions(r0, t_f)
        cols = []
        for j in range(N_COL):
            w = POOL_WINDOWS[j * LANES // GROUP_W]
            cur = geo.window(ext_ref, j, r0, 0)
            s = cur
            for i in range(1, w):
                s = s + geo.window(ext_ref, j, r0, i)
            inv_cnt = 1.0 / jnp.minimum(w, pos + 1).astype(jnp.float32)
            cols.append(s * inv_cnt - cur)
        p_ref[r0:r0 + ROW_BLK, :] = jnp.concatenate(cols, axis=1).astype(jnp.bfloat16)
        chain = _ordered_zero(functools.reduce(lambda a, b: a + b, cols))
    xs_ref[...] = x_ref[...].reshape(rows, d)

    geo.emit_history(ext_ref, newpool_ref)


def _resident(p, layer):
    if layer is None:
        zeros = (0,) * p.ndim
        return pl.BlockSpec(p.shape, lambda s: zeros, pipeline_mode=pl.Buffered(1))
    index = (layer,) + (0,) * (p.ndim - 1)
    return pl.BlockSpec((None,) + p.shape[1:], lambda s: index, pipeline_mode=pl.Buffered(1))


def _run_layer(body, x, prev, params, *, geo, n_tiles, nt, extra_scratch, name, **static):
    d = D_MODEL
    rows = geo.rows
    shared_prev = prev.shape[1] * prev.shape[2] == geo.hist and n_tiles > nt

    def front(s):
        return jnp.minimum(s, n_tiles - 1)

    def back(s):
        return jnp.maximum(s - 1, 0)

    x_spec = pl.BlockSpec(geo.x_block, lambda s: geo.x_index(front(s), nt))
    out_spec = pl.BlockSpec(geo.x_block, lambda s: geo.x_index(back(s), nt))
    prev_spec = pl.BlockSpec(geo.state_block,
                             (lambda s: (0, 0, 0, 0)) if shared_prev else (lambda s: geo.state_index(front(s), nt)))
    state_spec = pl.BlockSpec(geo.state_block, lambda s: geo.state_index(front(s), nt))

    scratch = [pltpu.VMEM(geo.ext_shape, jnp.float32),
               pltpu.VMEM((rows, d), jnp.float32),
               pltpu.VMEM((rows, d), jnp.float32)]
    scratch += [pltpu.VMEM((rows, width), jnp.bfloat16) for width in extra_scratch]

    def nbytes(shape, itemsize=4):
        n = itemsize * shape[-1] * (-(-shape[-2] // SUBLANES) * SUBLANES)
        for dim in shape[:-2]:
            n *= dim
        return n

    resident = sum((p.size if layer is None else p.size // p.shape[0]) * p.dtype.itemsize for p, layer in params)
    pipelined = 2 * (2 * nbytes(geo.x_block) + 2 * nbytes(geo.state_block))
    scratch_bytes = nbytes(geo.ext_shape) + 2 * rows * d * 4 + sum(rows * width * 2 for width in extra_scratch)
    temporaries = rows * D_FF * 4
    vmem_limit = min(resident + pipelined + scratch_bytes + temporaries, V7X_VMEM_BYTES - (4 << 20))

    return pl.pallas_call(
        functools.partial(body, geo=geo, nt=nt, n_tiles=n_tiles, **static),
        grid=(n_tiles + 1,),
        in_specs=[x_spec, prev_spec] + [_resident(p, layer) for p, layer in params],
        out_specs=(out_spec, state_spec),
        out_shape=(jax.ShapeDtypeStruct(x.shape, jnp.float32),
                   jax.ShapeDtypeStruct(geo.state_shape(n_tiles // nt), jnp.float32)),
        scratch_shapes=scratch,
        compiler_params=pltpu.CompilerParams(dimension_semantics=("arbitrary",),
                                             vmem_limit_bytes=int(vmem_limit)),
        name=name,
    )(x, prev, *[p for p, _ in params])


def _layer0(x, prev, params, *, geo, n_tiles, nt, name):
    return _run_layer(_layer0_kernel, x, prev, params, geo=geo, n_tiles=n_tiles, nt=nt,
                      extra_scratch=(D_MODEL, D_MODEL, D_MODEL, D_FF), name=name)


def _layer1(x, prev, params, *, geo, n_tiles, nt, pos0, name):
    return _run_layer(_layer1_kernel, x, prev, params, geo=geo, n_tiles=n_tiles, nt=nt,
                      extra_scratch=(D_MODEL, D_MODEL, D_FF), name=name, pos0=pos0)


PROMPT_TILE = 512
SAMPLE_SEQS_PER_TILE = 32


def kernel(x_prompt, x_sample, state_conv, state_pool, meta_tokens, norm_mix_g, norm_mlp_g, conv_w_pw1, conv_b_pw1, conv_w_dw, conv_b_dw, conv_ln_g, conv_ln_b, conv_w_pw2, conv_b_pw2, pool_w, pool_b, pool_scale, mlp_w1, mlp_w2, final_g):
    assert norm_mix_g.shape[0] == 2 and conv_w_pw1.shape[0] == 1 and pool_w.shape[0] == 1
    bf16 = jnp.bfloat16
    d = D_MODEL
    row = lambda a: (a.reshape(1, -1), None)
    w1, w2 = mlp_w1.astype(bf16), mlp_w2.astype(bf16)
    p0 = [row(norm_mix_g[0]), (conv_w_pw1.astype(bf16), 0), row(conv_b_pw1[0]), (conv_w_dw, 0),
          row(conv_b_dw[0]), row(conv_ln_g[0]), row(conv_ln_b[0]), (conv_w_pw2.astype(bf16), 0),
          row(conv_b_pw2[0]), row(norm_mlp_g[0]), (w1, 0), (w2, 0)]
    p1 = [row(norm_mix_g[1]), (pool_w.astype(bf16), 0), row(pool_b[0]), row(pool_scale[0]),
          row(norm_mlp_g[1]), (w1, 1), (w2, 1), row(final_g)]

    n_meta = meta_tokens.shape[0]
    n_prompt, seq, _ = x_prompt.shape
    n_sample, dec_seq, _ = x_sample.shape

    xm = meta_tokens.reshape(1, n_meta, d)
    xm1, conv_meta = _layer0(xm, jnp.zeros((1, 1, CONV_HIST, d), jnp.float32), p0,
                             geo=_SeqTile(n_meta, CONV_HIST), n_tiles=1, nt=1, name="meta_layer0")
    _, pool_meta = _layer1(xm1, jnp.zeros((1, 1, POOL_HIST, d), jnp.float32), p1,
                           geo=_SeqTile(n_meta, POOL_HIST), n_tiles=1, nt=1, pos0=0, name="meta_layer1")

    nt = seq // PROMPT_TILE
    xp1, conv_prompt = _layer0(x_prompt, conv_meta, p0, geo=_SeqTile(PROMPT_TILE, CONV_HIST),
                               n_tiles=n_prompt * nt, nt=nt, name="prompt_layer0")
    y_prompt, pool_prompt = _layer1(xp1, pool_meta, p1, geo=_SeqTile(PROMPT_TILE, POOL_HIST),
                                    n_tiles=n_prompt * nt, nt=nt, pos0=n_meta, name="prompt_layer1")

    sb = SAMPLE_SEQS_PER_TILE
    xs_t = jnp.transpose(x_sample, (1, 0, 2))
    xs1, conv_sample = _layer0(xs_t, jnp.transpose(state_conv, (0, 2, 1, 3)), p0,
                               geo=_TimeMajorTile(sb, dec_seq, CONV_HIST), n_tiles=n_sample // sb, nt=1,
                               name="sample_layer0")
    ys_t, pool_sample = _layer1(xs1, jnp.transpose(state_pool, (0, 2, 1, 3)), p1,
                                geo=_TimeMajorTile(sb, dec_seq, POOL_HIST), n_tiles=n_sample // sb, nt=1,
                                pos0=PAST_LEN, name="sample_layer1")

    return (y_prompt, jnp.transpose(ys_t, (1, 0, 2)), conv_prompt, jnp.transpose(conv_sample, (0, 2, 1, 3)),
            pool_prompt, jnp.transpose(pool_sample, (0, 2, 1, 3)))
```
